```python
import math
import jax, jax.numpy as jnp
from jax import lax
import numpy as np

D_MODEL = 1024
BATCH = 16
SEQ = 256
DEPTH = 4
DEC_BATCH = 2
DEC_SEQ = 4096
PAST_LEN = 256

GRID_W = 64
HEAD_DIM = 64
ATTN_Q_HEADS = 8
ATTN_KV_HEADS = 2
ATTN_GROUP = ATTN_Q_HEADS // ATTN_KV_HEADS
ATTN_WIDTH = ATTN_Q_HEADS * HEAD_DIM
ATTN_KV_WIDTH = ATTN_KV_HEADS * HEAD_DIM
ATTN_IN = ATTN_WIDTH + 2 * ATTN_KV_WIDTH
Q_BLOCK = 128
ROPE_THETA = 10000.0
ROPE_AXIS_DIM = HEAD_DIM // 2
RWKV_HEADS = 8
RWKV_HEAD = 64
RWKV_WIDTH = RWKV_HEADS * RWKV_HEAD
RWKV_DECAY_LORA = 64
RWKV_ICLR_LORA = 64
RWKV_GATE_LORA = 128
RWKV_IN = 3 * RWKV_WIDTH + RWKV_GATE_LORA + 2 * RWKV_DECAY_LORA + 2 * RWKV_ICLR_LORA
AB_IN = ATTN_IN + RWKV_IN
MIX_WIDTH = ATTN_WIDTH + RWKV_WIDTH
S5_WIDTH = D_MODEL
S5_GROUP_CH = 16
S5_GROUPS = S5_WIDTH // S5_GROUP_CH
S5_STATE = 64
D_FF = 2816
N_AB = (DEPTH + 1) // 2
N_C = DEPTH // 2
RMS_EPS = 1e-6
GN_EPS = 64e-5
F32 = jnp.float32

kernel_name = 'hybrid_prefix_diffusion_step'


def rms_norm(x, gain):
    x32 = x.astype(F32)
    y = x32 * lax.rsqrt(jnp.mean(x32 * x32, axis=-1, keepdims=True) + RMS_EPS)
    return y.astype(x.dtype) * gain


def swiglu(h, w1, w3, w2):
    return (jax.nn.silu(h @ w1) * (h @ w3)) @ w2


def adaln(cvec, ada_w, ada_b):
    m = jnp.einsum('bd,lde->ble', jax.nn.silu(cvec), ada_w) + ada_b
    return m.reshape(cvec.shape[0], DEPTH, 3, 3, D_MODEL)


def modulated_in(x, gain, m):
    return rms_norm(x, gain) * (1 + m[:, None, 1]) + m[:, None, 0]


def residual_out(x, y, gain, m, weight):
    return x + weight * m[:, None, 2] * rms_norm(y, gain)


def axial_rope(length):
    rows = length // GRID_W
    row = jnp.repeat(jnp.arange(rows), GRID_W).astype(F32)
    col = jnp.tile(jnp.arange(GRID_W), rows).astype(F32)
    inv = ROPE_THETA ** (-jnp.arange(ROPE_AXIS_DIM // 2, dtype=F32) * 2.0 / ROPE_AXIS_DIM)
    ar, ac = row[:, None] * inv, col[:, None] * inv
    ang = jnp.concatenate([ar, ar, ac, ac], axis=-1)
    return jnp.cos(ang), jnp.sin(ang)


def apply_rope(x, cos, sin):
    x1, x2, x3, x4 = jnp.split(x, 4, axis=-1)
    rot = jnp.concatenate([-x2, x1, -x4, x3], axis=-1)
    shape = (1, cos.shape[0]) + (1,) * (x.ndim - 3) + (HEAD_DIM,)
    y = x.astype(F32) * cos.reshape(shape) + rot.astype(F32) * sin.reshape(shape)
    return y.astype(x.dtype)


def blocked_attention(q, k, v):
    b, s = q.shape[:2]
    nb = s // Q_BLOCK
    qb = jnp.moveaxis(q.reshape(b, nb, Q_BLOCK, ATTN_KV_HEADS, ATTN_GROUP, HEAD_DIM), 1, 0)
    scale = HEAD_DIM ** -0.5

    def one_block(qblk):
        sc = jnp.einsum('bqkgd,btkd->bkgqt', qblk, k).astype(F32) * scale
        p = jax.nn.softmax(sc, axis=-1).astype(v.dtype)
        return jnp.einsum('bkgqt,btkd->bqkgd', p, v)

    o = lax.map(one_block, qb)
    return jnp.moveaxis(o, 0, 1).reshape(b, s, ATTN_WIDTH)


def attention_mixer(pa, q_gain, k_gain, rope, ctx_kv):
    b, L = pa.shape[:2]
    q, k, v = jnp.split(pa, [ATTN_WIDTH, ATTN_WIDTH + ATTN_KV_WIDTH], axis=-1)
    q = rms_norm(q.reshape(b, L, ATTN_KV_HEADS, ATTN_GROUP, HEAD_DIM), q_gain)
    k = rms_norm(k.reshape(b, L, ATTN_KV_HEADS, HEAD_DIM), k_gain)
    v = v.reshape(b, L, ATTN_KV_HEADS, HEAD_DIM)
    if ctx_kv is None:
        return blocked_attention(q, k, v), k, v
    cos, sin = rope
    qr = apply_rope(q, cos, sin)
    kr = apply_rope(k, cos, sin)
    keys = jnp.concatenate([kr, ctx_kv[0]], axis=1)
    vals = jnp.concatenate([v, ctx_kv[1]], axis=1)
    return blocked_attention(qr, keys, vals), k, v


def centred_shift_mix(p, mu):
    pad = jnp.pad(p, ((0, 0), (1, 1), (0, 0)))
    nb = 0.5 * (pad[:, :-2] + pad[:, 2:])
    return p + mu * (nb - p)


def wkv_scan(r, decay, k, v, kk, a, s0, reverse):
    def step(S, inp):
        r_t, w_t, k_t, v_t, kk_t, a_t = inp
        sa = jnp.einsum('bhvk,bhk->bhv', S, -kk_t)
        S = (S * w_t[:, :, None, :] + sa[..., None] * (kk_t * a_t)[:, :, None, :]
             + v_t[..., None] * k_t[:, :, None, :])
        return S, jnp.einsum('bhvk,bhk->bhv', S, r_t)

    xs = tuple(jnp.moveaxis(t, 1, 0) for t in (r, decay, k, v, kk, a))
    s_final, y = lax.scan(step, s0, xs, reverse=reverse)
    return jnp.moveaxis(y, 0, 1), s_final


def rwkv_mixer(pb, mu, w0, w_up, a0, a_up, g_up, k_k, k_a, r_k, ln_w, ln_b, s0):
    b, L = pb.shape[:2]
    pb = centred_shift_mix(pb, mu)
    sizes = [RWKV_WIDTH, RWKV_WIDTH, RWKV_WIDTH, RWKV_GATE_LORA,
             RWKV_DECAY_LORA, RWKV_DECAY_LORA, RWKV_ICLR_LORA]
    offs = np.cumsum(sizes).tolist()
    r, k, v, g_d, wd_f, wd_b, ad_f, ad_b = jnp.split(pb, offs, axis=-1)
    g = jax.nn.sigmoid(g_d) @ g_up

    def heads(t):
        return t.reshape(b, L, RWKV_HEADS, RWKV_HEAD).astype(F32)

    r32, k32, v32 = heads(r), heads(k), heads(v)
    kk = heads(k * k_k)
    kk = kk / jnp.maximum(jnp.sqrt(jnp.sum(kk * kk, axis=-1, keepdims=True)), 1e-12)
    k_a_h = k_a.reshape(RWKV_HEADS, RWKV_HEAD).astype(F32)
    r_k32 = r_k.astype(F32)
    s0 = s0.astype(F32)
    outs, finals = [], []
    for d, (wd, ad) in enumerate(((wd_f, ad_f), (wd_b, ad_b))):
        w = -jax.nn.softplus(-(w0[d] + jnp.tanh(wd) @ w_up[d]).astype(F32)) - 0.5
        decay = jnp.exp(-jnp.exp(heads(w)))
        a = jax.nn.sigmoid(heads(a0[d] + ad @ a_up[d]))
        kd = k32 * (1 + (a - 1) * k_a_h)
        yd, sd = wkv_scan(r32, decay, kd, v32, kk, a, s0[:, d], reverse=(d == 1))
        bonus = jnp.sum(r32 * kd * r_k32, axis=-1, keepdims=True) * v32
        outs.append(yd + bonus)
        finals.append(sd)
    y = outs[0] + outs[1]
    mean = jnp.mean(y, axis=-1, keepdims=True)
    var = jnp.mean(jnp.square(y - mean), axis=-1, keepdims=True)
    y = ((y - mean) * lax.rsqrt(var + GN_EPS)).reshape(b, L, RWKV_WIDTH).astype(pb.dtype)
    y = y * ln_w + ln_b
    return y * g, jnp.stack(finals, axis=1)


def s5_combine(e1, e2):
    a1r, a1i, b1r, b1i = e1
    a2r, a2i, b2r, b2i = e2
    return (a2r * a1r - a2i * a1i, a2r * a1i + a2i * a1r,
            a2r * b1r - a2i * b1i + b2r, a2r * b1i + a2i * b1r + b2i)


def s5_mixer(u, lam_re, lam_im, log_dt, b_re, b_im, c_re, c_im, d_skip, s0):
    bsz, L = u.shape[:2]
    ug = u.reshape(bsz, L, S5_GROUPS, S5_GROUP_CH).astype(F32)
    s0 = s0.astype(F32)
    ys, finals = [], []
    for d in range(2):
        lr, li = lam_re[d].astype(F32), lam_im[d].astype(F32)
        dt = jnp.exp(log_dt[d].astype(F32))[:, None]
        mag = jnp.exp(lr * dt)
        abr, abi = mag * jnp.cos(li * dt), mag * jnp.sin(li * dt)
        den = lr * lr + li * li
        fr = ((abr - 1) * lr + abi * li) / den
        fi = (abi * lr - (abr - 1) * li) / den
        br, bi = b_re[d].astype(F32), b_im[d].astype(F32)
        bbr = fr[..., None] * br - fi[..., None] * bi
        bbi = fr[..., None] * bi + fi[..., None] * br
        bur = jnp.einsum('blgc,gnc->blgn', ug, bbr)
        bui = jnp.einsum('blgc,gnc->blgn', ug, bbi)
        sr, si = s0[:, d, 0], s0[:, d, 1]
        first = 0 if d == 0 else L - 1
        bur = bur.at[:, first].add(abr * sr - abi * si)
        bui = bui.at[:, first].add(abr * si + abi * sr)
        ar = jnp.broadcast_to(abr, bur.shape)
        ai = jnp.broadcast_to(abi, bui.shape)
        _, _, hr, hi = lax.associative_scan(s5_combine, (ar, ai, bur, bui), reverse=(d == 1), axis=1)
        ys.append(jnp.einsum('blgn,gcn->blgc', hr, c_re[d].astype(F32))
                  - jnp.einsum('blgn,gcn->blgc', hi, c_im[d].astype(F32)))
        last = L - 1 if d == 0 else 0
        finals.append(jnp.stack([hr[:, last], hi[:, last]], axis=1))
    y = (ys[0] + ys[1]).reshape(bsz, L, S5_WIDTH) + d_skip.astype(F32) * u.astype(F32)
    return y.astype(u.dtype), jnp.stack(finals, axis=1)


def run_trunk(x, mods, P, ctx):
    b, L = x.shape[:2]
    rope = None if ctx is None else axial_rope(L)
    ks, vs, rws, s5s = [], [], [], []
    for l in range(DEPTH):
        m = mods[:, l]
        i = l // 2
        h = modulated_in(x, P['norm_pre'][l, 0], m[:, 0])
        f = swiglu(h, P['ffn_w1'][l, 0], P['ffn_w3'][l, 0], P['ffn_w2'][l, 0])
        x = residual_out(x, f, P['norm_post'][l, 0], m[:, 0], 0.5)
        h = modulated_in(x, P['norm_pre'][l, 1], m[:, 1])
        if l % 2 == 0:
            p = h @ P['ab_w_in'][i]
            pa, pb = p[..., :ATTN_IN], p[..., ATTN_IN:]
            if ctx is None:
                ckv = None
                s0 = jnp.zeros((b, 2, RWKV_HEADS, RWKV_HEAD, RWKV_HEAD), F32)
            else:
                ckv = (ctx[0][:, i], ctx[1][:, i])
                s0 = ctx[2][:, i]
            oa, k_c, v_c = attention_mixer(pa, P['attn_q_gain'][i], P['attn_k_gain'][i], rope, ckv)
            ob, s_fin = rwkv_mixer(pb, P['rwkv_mu'][i], P['rwkv_w0'][i], P['rwkv_w_up'][i],
                                   P['rwkv_a0'][i], P['rwkv_a_up'][i], P['rwkv_g_up'][i],
                                   P['rwkv_k_k'][i], P['rwkv_k_a'][i], P['rwkv_r_k'][i],
                                   P['rwkv_ln_w'][i], P['rwkv_ln_b'][i], s0)
            y = jnp.concatenate([oa, ob], axis=-1) @ P['ab_w_out'][i]
            if ctx is None:
                ks.append(k_c)
                vs.append(v_c)
                rws.append(s_fin)
        else:
            u = h @ P['s5_w_in'][i]
            if ctx is None:
                s0 = jnp.zeros((b, 2, 2, S5_GROUPS, S5_STATE), F32)
            else:
                s0 = ctx[3][:, i]
            ys, s_fin = s5_mixer(u, P['s5_lambda_re'][i], P['s5_lambda_im'][i], P['s5_log_dt'][i],
                                 P['s5_b_re'][i], P['s5_b_im'][i], P['s5_c_re'][i], P['s5_c_im'][i],
                                 P['s5_d'][i], s0)
            z = jax.nn.gelu(ys)
            z = z * jax.nn.sigmoid(z @ P['s5_w_glu'][i])
            y = z @ P['s5_w_out'][i]
            if ctx is None:
                s5s.append(s_fin)
        x = residual_out(x, y, P['norm_post'][l, 1], m[:, 1], 1.0)
        h = modulated_in(x, P['norm_pre'][l, 2], m[:, 2])
        f = swiglu(h, P['ffn_w1'][l, 1], P['ffn_w3'][l, 1], P['ffn_w2'][l, 1])
        x = residual_out(x, f, P['norm_post'][l, 2], m[:, 2], 0.5)
    if ctx is None:
        return x, (jnp.stack(ks, 1), jnp.stack(vs, 1), jnp.stack(rws, 1), jnp.stack(s5s, 1))
    return x, None


def setup_inputs(seed: int = 0) -> dict:
    key = jax.random.key(seed)
    ks = iter(jax.random.split(key, 64))

    def nrm(shape, s):
        return jax.random.normal(next(ks), shape, F32) * s

    def unif(shape, lo, hi):
        return jax.random.uniform(next(ks), shape, F32, lo, hi)

    D, E, F, G, N, C16 = D_MODEL, S5_WIDTH, D_FF, S5_GROUPS, S5_STATE, S5_GROUP_CH
    lam_im = jnp.pi * jnp.broadcast_to(jnp.arange(N, dtype=F32), (N_C, 2, G, N)) + nrm((N_C, 2, G, N), 0.01)
    return {
        'x_prompt': nrm((BATCH, SEQ, D), 1.0),
        'x_sample': nrm((DEC_BATCH, DEC_SEQ, D), 1.0),
        'cache_k': nrm((DEC_BATCH, N_AB, PAST_LEN, ATTN_KV_HEADS, HEAD_DIM), 1.0),
        'cache_v': nrm((DEC_BATCH, N_AB, PAST_LEN, ATTN_KV_HEADS, HEAD_DIM), 0.6),
        'state_rwkv': nrm((DEC_BATCH, N_AB, 2, RWKV_HEADS, RWKV_HEAD, RWKV_HEAD), 0.3),
        'state_s5': nrm((DEC_BATCH, N_C, 2, 2, G, N), 0.1),
        'c': nrm((DEC_BATCH, D), 1.0),
        'c_ctx': nrm((D,), 1.0),
        'ada_w': nrm((DEPTH, D, 9 * D), 0.5 * D ** -0.5),
        'ada_b': nrm((DEPTH, 9 * D), 0.02),
        'norm_pre': 1.0 + nrm((DEPTH, 3, D), 0.02),
        'norm_post': 1.0 + nrm((DEPTH, 3, D), 0.02),
        'ffn_w1': nrm((DEPTH, 2, D, F), D ** -0.5),
        'ffn_w3': nrm((DEPTH, 2, D, F), D ** -0.5),
        'ffn_w2': nrm((DEPTH, 2, F, D), F ** -0.5),
        'ab_w_in': nrm((N_AB, D, AB_IN), D ** -0.5),
        'ab_w_out': nrm((N_AB, MIX_WIDTH, D), MIX_WIDTH ** -0.5),
        'attn_q_gain': 1.0 + nrm((N_AB, HEAD_DIM), 0.02),
        'attn_k_gain': 1.0 + nrm((N_AB, HEAD_DIM), 0.02),
        'rwkv_mu': unif((N_AB, RWKV_IN), 0.0, 1.0),
        'rwkv_w0': unif((N_AB, 2, RWKV_WIDTH), -5.0, 0.0),
        'rwkv_w_up': nrm((N_AB, 2, RWKV_DECAY_LORA, RWKV_WIDTH), 0.5 * RWKV_DECAY_LORA ** -0.5),
        'rwkv_a0': nrm((N_AB, 2, RWKV_WIDTH), 0.1),
        'rwkv_a_up': nrm((N_AB, 2, RWKV_ICLR_LORA, RWKV_WIDTH), 0.5 * RWKV_ICLR_LORA ** -0.5),
        'rwkv_g_up': nrm((N_AB, RWKV_GATE_LORA, RWKV_WIDTH), RWKV_GATE_LORA ** -0.5),
        'rwkv_k_k': 0.85 + nrm((N_AB, RWKV_WIDTH), 0.02),
        'rwkv_k_a': 1.0 + nrm((N_AB, RWKV_WIDTH), 0.02),
        'rwkv_r_k': nrm((N_AB, RWKV_HEADS, RWKV_HEAD), 0.1),
        'rwkv_ln_w': 1.0 + nrm((N_AB, RWKV_WIDTH), 0.02),
        'rwkv_ln_b': nrm((N_AB, RWKV_WIDTH), 0.02),
        's5_w_in': nrm((N_C, D, E), D ** -0.5),
        's5_lambda_re': -0.5 + nrm((N_C, 2, G, N), 0.01),
        's5_lambda_im': lam_im,
        's5_log_dt': unif((N_C, 2, G), math.log(0.001), math.log(0.1)),
        's5_b_re': nrm((N_C, 2, G, N, C16), (2 * C16) ** -0.5),
        's5_b_im': nrm((N_C, 2, G, N, C16), (2 * C16) ** -0.5),
        's5_c_re': nrm((N_C, 2, G, C16, N), (2 * N) ** -0.5),
        's5_c_im': nrm((N_C, 2, G, C16, N), (2 * N) ** -0.5),
        's5_d': nrm((N_C, E), 1.0),
        's5_w_glu': nrm((N_C, E, E), E ** -0.5),
        's5_w_out': nrm((N_C, E, D), E ** -0.5),
    }


def reference(x_prompt, x_sample, cache_k, cache_v, state_rwkv, state_s5, c, c_ctx,
              ada_w, ada_b, norm_pre, norm_post, ffn_w1, ffn_w3, ffn_w2,
              ab_w_in, ab_w_out, attn_q_gain, attn_k_gain,
              rwkv_mu, rwkv_w0, rwkv_w_up, rwkv_a0, rwkv_a_up, rwkv_g_up,
              rwkv_k_k, rwkv_k_a, rwkv_r_k, rwkv_ln_w, rwkv_ln_b,
              s5_w_in, s5_lambda_re, s5_lambda_im, s5_log_dt, s5_b_re, s5_b_im,
              s5_c_re, s5_c_im, s5_d, s5_w_glu, s5_w_out):
    P = dict(norm_pre=norm_pre, norm_post=norm_post, ffn_w1=ffn_w1, ffn_w3=ffn_w3, ffn_w2=ffn_w2,
             ab_w_in=ab_w_in, ab_w_out=ab_w_out, attn_q_gain=attn_q_gain, attn_k_gain=attn_k_gain,
             rwkv_mu=rwkv_mu, rwkv_w0=rwkv_w0, rwkv_w_up=rwkv_w_up, rwkv_a0=rwkv_a0,
             rwkv_a_up=rwkv_a_up, rwkv_g_up=rwkv_g_up, rwkv_k_k=rwkv_k_k, rwkv_k_a=rwkv_k_a,
             rwkv_r_k=rwkv_r_k, rwkv_ln_w=rwkv_ln_w, rwkv_ln_b=rwkv_ln_b,
             s5_w_in=s5_w_in, s5_lambda_re=s5_lambda_re, s5_lambda_im=s5_lambda_im,
             s5_log_dt=s5_log_dt, s5_b_re=s5_b_re, s5_b_im=s5_b_im, s5_c_re=s5_c_re,
             s5_c_im=s5_c_im, s5_d=s5_d, s5_w_glu=s5_w_glu, s5_w_out=s5_w_out)
    mods_ctx = adaln(c_ctx[None], ada_w, ada_b)
    mods_lat = adaln(c, ada_w, ada_b)
    y_prompt, ctx_out = run_trunk(x_prompt, mods_ctx, P, None)
    new_k, new_v, new_rwkv, new_s5 = ctx_out
    y_sample, _ = run_trunk(x_sample, mods_lat, P, (cache_k, cache_v, state_rwkv, state_s5))
    return (y_prompt, y_sample, new_k, new_v, new_rwkv, new_s5)
```

```python
import functools
import math

import numpy as np
import jax
import jax.numpy as jnp
from jax import lax
from jax.experimental import pallas as pl
from jax.experimental.pallas import tpu as pltpu

F32 = jnp.float32
BF16 = jnp.bfloat16

D_MODEL = 1024
DEPTH = 4
GRID_W = 64
HEAD_DIM = 64
ATTN_Q_HEADS = 8
ATTN_KV_HEADS = 2
ATTN_WIDTH = ATTN_Q_HEADS * HEAD_DIM
ATTN_KV_WIDTH = ATTN_KV_HEADS * HEAD_DIM
ATTN_IN = ATTN_WIDTH + 2 * ATTN_KV_WIDTH
Q_BLOCK = 128
ROPE_THETA = 10000.0
RWKV_HEADS = 8
RWKV_HEAD = 64
RWKV_WIDTH = RWKV_HEADS * RWKV_HEAD
RWKV_IN = 3 * RWKV_WIDTH + 128 + 2 * 64 + 2 * 64
AB_IN = ATTN_IN + RWKV_IN
S5_GROUP_CH = 16
S5_GROUPS = D_MODEL // S5_GROUP_CH
S5_STATE = 64
S5_LANES = S5_GROUPS * S5_STATE
D_FF = 2816
RMS_EPS = 1e-6
GN_EPS = 64e-5

VMEM_LIMIT_V7X = 56 * 1024 * 1024
WKV_CHUNK = 64
S5_CHUNK = 256
S5_SEGS = 8


def _cparams(sem):
    return pltpu.CompilerParams(dimension_semantics=sem, vmem_limit_bytes=VMEM_LIMIT_V7X)


def _bdot(a, b):
    return jnp.dot(a.astype(BF16), b.astype(BF16), preferred_element_type=F32)


def _bdot_nt(a, b):
    return lax.dot_general(a.astype(BF16), b.astype(BF16), (((1,), (1,)), ((), ())),
                           preferred_element_type=F32)


def _bdot_tn(a, b):
    return lax.dot_general(a.astype(BF16), b.astype(BF16), (((0,), (0,)), ((), ())),
                           preferred_element_type=F32)


def _split2(x):
    hi = x.astype(BF16)
    lo = (x - hi.astype(F32)).astype(BF16)
    return hi, lo


def _split3(x):
    hi = x.astype(BF16)
    r = x - hi.astype(F32)
    mid = r.astype(BF16)
    lo = (r - mid.astype(F32)).astype(BF16)
    return hi, mid, lo


def _dot3(a, b):
    ah, al = _split2(a)
    bh, bl = _split2(b)
    d = functools.partial(jnp.dot, preferred_element_type=F32)
    return d(ah, bh) + (d(ah, bl) + d(al, bh))


def _segsum(x, ones_bd):
    hi, lo = _split2(x)
    d = functools.partial(jnp.dot, preferred_element_type=F32)
    return d(hi, ones_bd) + d(lo, ones_bd)


def _rms(x):
    return x * lax.rsqrt(jnp.mean(x * x, axis=-1, keepdims=True) + RMS_EPS)


def _silu(x):
    return x * jax.nn.sigmoid(x)


def _adaln_kernel(c_ref, w_ref, b_ref, o_ref):
    o_ref[...] = _bdot(_silu(c_ref[...]), w_ref[...]) + b_ref[...]


def _adaln(cvec8, ada_w, ada_b, tn=2304):
    depth, d, e = ada_w.shape
    return pl.pallas_call(
        _adaln_kernel,
        grid=(depth, e // tn),
        in_specs=[
            pl.BlockSpec((8, d), lambda l, j: (0, 0)),
            pl.BlockSpec((None, d, tn), lambda l, j: (l, 0, j)),
            pl.BlockSpec((None, 1, tn), lambda l, j: (l, 0, j)),
        ],
        out_specs=pl.BlockSpec((None, 8, tn), lambda l, j: (l, 0, j)),
        out_shape=jax.ShapeDtypeStruct((depth, 8, e), F32),
        compiler_params=_cparams(("parallel", "parallel")),
        name="adaln",
    )(cvec8, ada_w, ada_b.reshape(depth, 1, e))


class _Groups:
    def __init__(self, n_prompt, len_prompt, n_sample, len_sample):
        self.bp, self.lp, self.bs, self.ls = n_prompt, len_prompt, n_sample, len_sample
        self.np_rows = n_prompt * len_prompt
        self.n_rows = self.np_rows + n_sample * len_sample

    def mod_row(self, row0):
        return jnp.where(row0 < self.np_rows, 0, 1 + (row0 - self.np_rows) // self.ls)


def _mod_specs(l, s, kinds):
    return [pl.BlockSpec((None, 8, D_MODEL), functools.partial(lambda k, i, *_: (l, 0, s * 3 + k), k))
            for k in kinds]


def _modulated(x, gpre, sh_ref, sc_ref, g):
    return _rms(x) * gpre * (1 + sc_ref[pl.ds(g, 1), :]) + sh_ref[pl.ds(g, 1), :]


def _residual(x, y, gpost, gt_ref, g, weight):
    return x + (weight * gt_ref[pl.ds(g, 1), :]) * (_rms(y) * gpost)


def _ffn_kernel(x_ref, sh_ref, sc_ref, gt_ref, gpre_ref, gpost_ref, w1_ref, w3_ref, w2_ref,
                o_ref, h_scr, acc_scr, *, groups, tm, nf):
    j = pl.program_id(1)
    g = groups.mod_row(pl.program_id(0) * tm)

    @pl.when(j == 0)
    def _():
        h_scr[...] = _modulated(x_ref[...], gpre_ref[...], sh_ref, sc_ref, g).astype(BF16)
        acc_scr[...] = jnp.zeros_like(acc_scr)

    h = h_scr[...]
    a = jnp.dot(h, w1_ref[...].astype(BF16), preferred_element_type=F32)
    b = jnp.dot(h, w3_ref[...].astype(BF16), preferred_element_type=F32)
    acc_scr[...] += _bdot(_silu(a) * b, w2_ref[...])

    @pl.when(j == nf - 1)
    def _():
        o_ref[...] = _residual(x_ref[...], acc_scr[...], gpost_ref[...], gt_ref, g, 0.5)


def _ffn(x, mods, l, s, gpre, gpost, w1, w3, w2, groups, tm, tf=256):
    n, d = x.shape
    f = w1.shape[1]
    nf = f // tf
    row = lambda i, j: (i, 0)
    const = lambda i, j: (0, 0)
    return pl.pallas_call(
        functools.partial(_ffn_kernel, groups=groups, tm=tm, nf=nf),
        grid=(n // tm, nf),
        in_specs=[pl.BlockSpec((tm, d), row)] + _mod_specs(l, s, (0, 1, 2)) + [
            pl.BlockSpec((1, d), const), pl.BlockSpec((1, d), const),
            pl.BlockSpec((d, tf), lambda i, j: (0, j)),
            pl.BlockSpec((d, tf), lambda i, j: (0, j)),
            pl.BlockSpec((tf, d), lambda i, j: (j, 0)),
        ],
        out_specs=pl.BlockSpec((tm, d), row),
        out_shape=jax.ShapeDtypeStruct((n, d), F32),
        scratch_shapes=[pltpu.VMEM((tm, d), BF16), pltpu.VMEM((tm, d), F32)],
        compiler_params=_cparams(("parallel", "arbitrary")),
        name="ffn",
    )(x, mods, mods, mods, gpre.reshape(1, d), gpost.reshape(1, d), w1, w3, w2)


def _rope(x, cos, sin_signed):
    n = x.shape[1]
    lane = lax.broadcasted_iota(jnp.int32, x.shape, 1)
    first = (lane & 31) < 16
    rot = jnp.where(first, pltpu.roll(x, n - 16, axis=1), pltpu.roll(x, 16, axis=1))
    return x * cos + rot * sin_signed


def _abin_kernel(x_ref, sh_ref, sc_ref, gpre_ref, w_ref, qg_ref, kg_ref, cos_ref, sin_ref, bd_ref,
                 q_ref, k_ref, v_ref, pb_ref, *, groups, tm):
    g = groups.mod_row(pl.program_id(0) * tm)
    h = _modulated(x_ref[...], gpre_ref[...], sh_ref, sc_ref, g).astype(BF16)
    cos = cos_ref[...]
    sin = sin_ref[...]
    inv_hd = 1.0 / HEAD_DIM

    pq = jnp.dot(h, w_ref[:, 0:ATTN_WIDTH].astype(BF16), preferred_element_type=F32)
    ssq = _segsum(pq * pq, bd_ref[...])
    qn = pq * lax.rsqrt(ssq * inv_hd + RMS_EPS) * qg_ref[...]
    q_ref[...] = _rope(qn, jnp.concatenate([cos] * 4, axis=1), jnp.concatenate([sin] * 4, axis=1))

    k0 = ATTN_WIDTH
    pk = jnp.dot(h, w_ref[:, k0:k0 + ATTN_KV_WIDTH].astype(BF16), preferred_element_type=F32)
    ssk = _segsum(pk * pk, bd_ref[0:ATTN_KV_WIDTH, 0:ATTN_KV_WIDTH])
    kn = pk * lax.rsqrt(ssk * inv_hd + RMS_EPS) * kg_ref[...]
    k_ref[...] = _rope(kn, cos, sin)

    v0 = k0 + ATTN_KV_WIDTH
    v_ref[...] = jnp.dot(h, w_ref[:, v0:v0 + ATTN_KV_WIDTH].astype(BF16), preferred_element_type=F32)
    pb_ref[...] = jnp.dot(h, w_ref[:, ATTN_IN:AB_IN].astype(BF16), preferred_element_type=F32)


def _abin(x, mods, l, gpre, w_in, q_gain, k_gain, cos_t, sin_t, ones_bd, groups, tm):
    n, d = x.shape
    row = lambda i: (i, 0)
    const = lambda i: (0, 0)
    return pl.pallas_call(
        functools.partial(_abin_kernel, groups=groups, tm=tm),
        grid=(n // tm,),
        in_specs=[pl.BlockSpec((tm, d), row)] + _mod_specs(l, 1, (0, 1)) + [
            pl.BlockSpec((1, d), const),
            pl.BlockSpec((d, AB_IN), const),
            pl.BlockSpec((1, ATTN_WIDTH), const),
            pl.BlockSpec((1, ATTN_KV_WIDTH), const),
            pl.BlockSpec((tm, 128), row),
            pl.BlockSpec((tm, 128), row),
            pl.BlockSpec((RWKV_WIDTH, RWKV_WIDTH), const),
        ],
        out_specs=[pl.BlockSpec((tm, ATTN_WIDTH), row), pl.BlockSpec((tm, ATTN_KV_WIDTH), row),
                   pl.BlockSpec((tm, ATTN_KV_WIDTH), row), pl.BlockSpec((tm, RWKV_IN), row)],
        out_shape=[jax.ShapeDtypeStruct((n, ATTN_WIDTH), F32),
                   jax.ShapeDtypeStruct((n, ATTN_KV_WIDTH), F32),
                   jax.ShapeDtypeStruct((n, ATTN_KV_WIDTH), F32),
                   jax.ShapeDtypeStruct((n, RWKV_IN), F32)],
        compiler_params=_cparams(("parallel",)),
        name="ab_in",
    )(x, mods, mods, gpre.reshape(1, d), w_in, jnp.tile(q_gain, ATTN_Q_HEADS).reshape(1, -1),
      jnp.tile(k_gain, ATTN_KV_HEADS).reshape(1, -1), cos_t, sin_t, ones_bd)


def _attn_kernel(*refs, key_chunks):
    q_ref, o_ref = refs[0], refs[-1]
    kv_refs = refs[1:-1]
    tq = q_ref.shape[0]
    group = ATTN_Q_HEADS // ATTN_KV_HEADS
    lane = lax.broadcasted_iota(jnp.int32, (tq, 128), 1)
    hi_half = lane >= HEAD_DIM
    out_chunks = [None] * (ATTN_WIDTH // 128)
    for kv in range(ATTN_KV_HEADS):
        keep = hi_half if kv == 1 else jnp.logical_not(hi_half)
        parts = []
        for gi in range(group):
            hd = kv * group + gi
            chunk = q_ref[:, (hd // 2) * 128:(hd // 2 + 1) * 128]
            if hd % 2 != kv:
                chunk = pltpu.roll(chunk, HEAD_DIM, axis=1)
            parts.append(jnp.where(keep, chunk * (HEAD_DIM ** -0.5), 0.0).astype(BF16))
        qs = jnp.concatenate(parts, axis=0)
        m = jnp.full((group * tq, 1), -jnp.inf, F32)
        den = jnp.zeros((group * tq, 1), F32)
        acc = jnp.zeros((group * tq, 128), F32)
        for src, start, size in key_chunks:
            k_ref, v_ref = kv_refs[2 * src], kv_refs[2 * src + 1]
            s = _bdot_nt(qs, k_ref[start:start + size, :])
            m_new = jnp.maximum(m, jnp.max(s, axis=1, keepdims=True))
            alpha = jnp.exp(m - m_new)
            p = jnp.exp(s - m_new)
            den = alpha * den + jnp.sum(p, axis=1, keepdims=True)
            acc = alpha * acc + _bdot(p, v_ref[start:start + size, :])
            m = m_new
        o = acc / den
        for gi in range(group):
            hd = kv * group + gi
            piece = o[gi * tq:(gi + 1) * tq, :]
            if hd % 2 != kv:
                piece = pltpu.roll(piece, HEAD_DIM, axis=1)
            c = hd // 2
            if out_chunks[c] is None:
                out_chunks[c] = piece
            else:
                out_chunks[c] = jnp.where(hi_half, piece, out_chunks[c])
    for c, val in enumerate(out_chunks):
        o_ref[:, c * 128:(c + 1) * 128] = val


def _attention(q, k, v, ctx_k, ctx_v, row0, nseq, seqlen, key_chunk=1024):
    nq = seqlen // Q_BLOCK
    qb0 = row0 // Q_BLOCK
    sb0 = row0 // seqlen
    chunks = [(0, s, min(key_chunk, seqlen - s)) for s in range(0, seqlen, key_chunk)]
    args = [q, k, v]
    in_specs = [pl.BlockSpec((Q_BLOCK, ATTN_WIDTH), lambda b, i: (qb0 + b * nq + i, 0)),
                pl.BlockSpec((seqlen, ATTN_KV_WIDTH), lambda b, i: (sb0 + b, 0)),
                pl.BlockSpec((seqlen, ATTN_KV_WIDTH), lambda b, i: (sb0 + b, 0))]
    if ctx_k is not None:
        past = ctx_k.shape[1]
        chunks.append((1, 0, past))
        args += [ctx_k, ctx_v]
        in_specs += [pl.BlockSpec((None, past, ATTN_KV_WIDTH), lambda b, i: (b, 0, 0))] * 2
    return pl.pallas_call(
        functools.partial(_attn_kernel, key_chunks=tuple(chunks)),
        grid=(nseq, nq),
        in_specs=in_specs,
        out_specs=pl.BlockSpec((Q_BLOCK, ATTN_WIDTH), lambda b, i: (b * nq + i, 0)),
        out_shape=jax.ShapeDtypeStruct((nseq * seqlen, ATTN_WIDTH), F32),
        compiler_params=_cparams(("parallel", "arbitrary")),
        name="attn",
    )(*args)


def _rwkv_prep_kernel(pb_ref, prev_ref, next_ref, mu_ref, w0_ref, wup_ref, a0_ref, aup_ref, gup_ref,
                      kk_ref, ka_ref, rk_ref, bd_ref,
                      r_o, v_o, kk_o, g_o, bonus_o, lwf_o, lwb_o, kdf_o, kdb_o, bf_o, bb_o,
                      *, groups, tm):
    i = pl.program_id(0)
    row0 = i * tm
    seqlen = jnp.where(row0 < groups.np_rows, groups.lp, groups.ls)
    pos0 = jnp.where(row0 < groups.np_rows, row0 % groups.lp, (row0 - groups.np_rows) % groups.ls)
    first = pos0 == 0
    last = pos0 + tm == seqlen
    pb = pb_ref[...]
    prev_row = jnp.where(first, 0.0, prev_ref[7:8, :])
    next_row = jnp.where(last, 0.0, next_ref[0:1, :])
    rows = lax.broadcasted_iota(jnp.int32, pb.shape, 0)
    xm1 = jnp.where(rows == 0, prev_row, pltpu.roll(pb, 1, axis=0))
    xp1 = jnp.where(rows == tm - 1, next_row, pltpu.roll(pb, tm - 1, axis=0))
    xs = pb + mu_ref[...] * (0.5 * (xm1 + xp1) - pb)

    w = RWKV_WIDTH
    r = xs[:, 0:w]
    k = xs[:, w:2 * w]
    v = xs[:, 2 * w:3 * w]
    g_d = xs[:, 3 * w:3 * w + 128]
    wd = xs[:, 3 * w + 128:3 * w + 256]
    ad = xs[:, 3 * w + 256:3 * w + 384]
    bd = bd_ref[...]

    kk = k * kk_ref[...]
    kk = kk / jnp.maximum(jnp.sqrt(_segsum(kk * kk, bd)), 1e-12)
    tanh_wd = jnp.tanh(wd)
    kd_sum = None
    for d, (lw_o, kd_o, b_o) in enumerate(((lwf_o, kdf_o, bf_o), (lwb_o, kdb_o, bb_o))):
        z = w0_ref[d:d + 1, :] + _bdot(tanh_wd, wup_ref[d])
        softplus_neg = jnp.maximum(-z, 0.0) + jnp.log(1.0 + jnp.exp(-jnp.abs(z)))
        lw_o[...] = -jnp.exp(-softplus_neg - 0.5)
        a = jax.nn.sigmoid(a0_ref[d:d + 1, :] + _bdot(ad, aup_ref[d]))
        kd = k * (1 + (a - 1) * ka_ref[...])
        kd_o[...] = kd
        b_o[...] = kk * a
        kd_sum = kd if kd_sum is None else kd_sum + kd
    r_o[...] = r
    v_o[...] = v
    kk_o[...] = kk
    g_o[...] = _bdot(jax.nn.sigmoid(g_d), gup_ref[...])
    bonus_o[...] = _segsum(r * kd_sum * rk_ref[...], bd) * v


def _rwkv_prep(pb, mu, w0, w_up, a0, a_up, g_up, k_k, k_a, r_k, ones_bd, groups, tm):
    n = pb.shape[0]
    w = RWKV_WIDTH
    hb = tm // 8
    nb8 = n // 8
    row = lambda i: (i, 0)
    const2 = lambda i: (0, 0)
    const3 = lambda i: (0, 0, 0)
    zeros = jnp.zeros((64, w), F32)
    wup_pad = jnp.stack([jnp.concatenate([w_up[0], zeros], 0), jnp.concatenate([zeros, w_up[1]], 0)])
    aup_pad = jnp.stack([jnp.concatenate([a_up[0], zeros], 0), jnp.concatenate([zeros, a_up[1]], 0)])
    outs = pl.pallas_call(
        functools.partial(_rwkv_prep_kernel, groups=groups, tm=tm),
        grid=(n // tm,),
        in_specs=[
            pl.BlockSpec((tm, RWKV_IN), row),
            pl.BlockSpec((8, RWKV_IN), lambda i: (jnp.maximum(i * hb - 1, 0), 0)),
            pl.BlockSpec((8, RWKV_IN), lambda i: (jnp.minimum((i + 1) * hb, nb8 - 1), 0)),
            pl.BlockSpec((1, RWKV_IN), const2),
            pl.BlockSpec((2, w), const2),
            pl.BlockSpec((2, 128, w), const3),
            pl.BlockSpec((2, w), const2),
            pl.BlockSpec((2, 128, w), const3),
            pl.BlockSpec((128, w), const2),
            pl.BlockSpec((1, w), const2),
            pl.BlockSpec((1, w), const2),
            pl.BlockSpec((1, w), const2),
            pl.BlockSpec((w, w), const2),
        ],
        out_specs=[pl.BlockSpec((tm, w), row)] * 11,
        out_shape=[jax.ShapeDtypeStruct((n, w), F32)] * 11,
        compiler_params=_cparams(("parallel",)),
        name="rwkv_prep",
    )(pb, pb, pb, mu.reshape(1, -1), w0, wup_pad, a0, aup_pad, g_up, k_k.reshape(1, -1),
      k_a.reshape(1, -1), r_k.reshape(1, -1), ones_bd)
    return outs


def _inv_unit_minus_eye(n_mat, row, col, mm):
    size = n_mat.shape[0]
    same = lambda blk: (row // blk) == (col // blk)
    nd = jnp.where(same(8), n_mat, 0.0)
    n2 = mm(nd, nd)
    n4 = mm(n2, n2)
    e = n2 - nd - mm(nd, n2)
    e = e + n4 + mm(e, n4)
    blk = 16
    while blk <= size:
        noff = jnp.where(jnp.logical_and(same(blk), jnp.logical_not(same(blk // 2))), n_mat, 0.0)
        w = noff + mm(e, noff)
        e = e - (w + mm(w, e))
        blk *= 2
    return e


def _wkv_kernel(r_ref, v_ref, kk_ref, lw_ref, kd_ref, b_ref, s0_ref, y_ref, sfin_ref, s_scr,
                *, reverse, nchunk):
    c = pl.program_id(1)
    csz = lw_ref.shape[0]
    hs = RWKV_HEAD

    @pl.when(c == 0)
    def _():
        s_scr[...] = s0_ref[...]

    row = lax.broadcasted_iota(jnp.int32, (csz, csz), 0)
    col = lax.broadcasted_iota(jnp.int32, (csz, csz), 1)
    if reverse:
        strict, incl = col > row, col >= row
    else:
        strict, incl = col < row, col <= row

    lw = lw_ref[...]
    ones_tri = jnp.where(incl, 1.0, 0.0).astype(BF16)
    cum = functools.reduce(lambda a, b: a + b,
                           [jnp.dot(ones_tri, p, preferred_element_type=F32) for p in _split3(lw)])
    tot = cum[0:1, :] if reverse else cum[csz - 1:csz, :]
    e_neg = jnp.exp(-cum)
    e_rem = jnp.exp(tot - cum)
    kt = kk_ref[...] * jnp.exp(cum - lw)
    rt = r_ref[...] * jnp.exp(cum)
    bt = b_ref[...] * e_neg
    kdt = kd_ref[...] * e_neg
    bh = b_ref[...] * e_rem
    kh = kd_ref[...] * e_rem
    g_tot = jnp.exp(tot)
    v_all = v_ref[...]

    for h in range(RWKV_HEADS):
        sl = slice(h * hs, (h + 1) * hs)
        kt_h, rt_h, v_h = kt[:, sl], rt[:, sl], v_all[:, sl]
        gm = _bdot_nt(jnp.concatenate([kt_h, rt_h], axis=0),
                      jnp.concatenate([bt[:, sl], kdt[:, sl]], axis=0))
        a_ab = jnp.where(strict, gm[:csz, :csz], 0.0)
        a_ak = jnp.where(strict, gm[:csz, csz:], 0.0)
        r_ab = jnp.where(incl, gm[csz:, :csz], 0.0)
        r_ak = jnp.where(incl, gm[csz:, csz:], 0.0)
        e = _inv_unit_minus_eye(a_ab, row, col, _dot3)
        av = _bdot(a_ak, v_h)
        kt_t = kt_h + _dot3(e, kt_h)
        u = av + _dot3(e, av)
        s_old = s_scr[h]
        x = _bdot_nt(jnp.concatenate([kt_t, rt_h], axis=0), s_old)
        sa = -(x[:csz] + u)
        sv = jnp.concatenate([sa, v_h], axis=0)
        y_ref[:, sl] = x[csz:] + _bdot(jnp.concatenate([r_ab, r_ak], axis=1), sv)
        s_scr[h] = s_old * g_tot[:, sl] + _bdot_tn(sv, jnp.concatenate([bh[:, sl], kh[:, sl]], axis=0))

    @pl.when(c == nchunk - 1)
    def _():
        sfin_ref[...] = s_scr[...]


def _wkv(r, v, kk, lw, kd, b, s0, row0, nseq, seqlen, reverse):
    csz = WKV_CHUNK
    nchunk = seqlen // csz
    cb0 = row0 // csz
    w = RWKV_WIDTH
    if reverse:
        tok = lambda s, c: (cb0 + s * nchunk + nchunk - 1 - c, 0)
        out = lambda s, c: (s * nchunk + nchunk - 1 - c, 0)
    else:
        tok = lambda s, c: (cb0 + s * nchunk + c, 0)
        out = lambda s, c: (s * nchunk + c, 0)
    st = lambda s, c: (s, 0, 0, 0)
    st_spec = pl.BlockSpec((None, RWKV_HEADS, RWKV_HEAD, RWKV_HEAD), st)
    return pl.pallas_call(
        functools.partial(_wkv_kernel, reverse=reverse, nchunk=nchunk),
        grid=(nseq, nchunk),
        in_specs=[pl.BlockSpec((csz, w), tok)] * 6 + [st_spec],
        out_specs=[pl.BlockSpec((csz, w), out), st_spec],
        out_shape=[jax.ShapeDtypeStruct((nseq * seqlen, w), F32),
                   jax.ShapeDtypeStruct((nseq, RWKV_HEADS, RWKV_HEAD, RWKV_HEAD), F32)],
        scratch_shapes=[pltpu.VMEM((RWKV_HEADS, RWKV_HEAD, RWKV_HEAD), F32)],
        compiler_params=_cparams(("parallel", "arbitrary")),
        name="wkv_bwd" if reverse else "wkv_fwd",
    )(r, v, kk, lw, kd, b, s0)


def _about_kernel(x_ref, gt_ref, gpost_ref, oa_ref, yf_ref, yb_ref, bonus_ref, g_ref, lnw_ref, lnb_ref,
                  bd_ref, w_ref, o_ref, *, groups, tm):
    g = groups.mod_row(pl.program_id(0) * tm)
    bd = bd_ref[...]
    y = (yf_ref[...] + yb_ref[...]) + bonus_ref[...]
    inv_n = 1.0 / RWKV_HEAD
    dev = y - _segsum(y, bd) * inv_n
    yn = dev * lax.rsqrt(_segsum(dev * dev, bd) * inv_n + GN_EPS)
    ob = (yn * lnw_ref[...] + lnb_ref[...]) * g_ref[...]
    mix = _bdot(oa_ref[...], w_ref[0:ATTN_WIDTH, :]) + _bdot(ob, w_ref[ATTN_WIDTH:, :])
    o_ref[...] = _residual(x_ref[...], mix, gpost_ref[...], gt_ref, g, 1.0)


def _about(x, mods, l, gpost, oa, yf, yb, bonus, gate, ln_w, ln_b, ones_bd, w_out, groups, tm):
    n, d = x.shape
    w = RWKV_WIDTH
    row = lambda i: (i, 0)
    const = lambda i: (0, 0)
    return pl.pallas_call(
        functools.partial(_about_kernel, groups=groups, tm=tm),
        grid=(n // tm,),
        in_specs=[pl.BlockSpec((tm, d), row)] + _mod_specs(l, 1, (2,)) + [
            pl.BlockSpec((1, d), const),
            pl.BlockSpec((tm, ATTN_WIDTH), row)] + [pl.BlockSpec((tm, w), row)] * 4 + [
            pl.BlockSpec((1, w), const), pl.BlockSpec((1, w), const),
            pl.BlockSpec((w, w), const),
            pl.BlockSpec((ATTN_WIDTH + w, d), const)],
        out_specs=pl.BlockSpec((tm, d), row),
        out_shape=jax.ShapeDtypeStruct((n, d), F32),
        compiler_params=_cparams(("parallel",)),
        name="ab_out",
    )(x, mods, gpost.reshape(1, d), oa, yf, yb, bonus, gate, ln_w.reshape(1, w), ln_b.reshape(1, w),
      ones_bd, w_out)


def _s5in_kernel(x_ref, sh_ref, sc_ref, gpre_ref, w_ref, u_ref, *, groups, tm):
    g = groups.mod_row(pl.program_id(0) * tm)
    u_ref[...] = _bdot(_modulated(x_ref[...], gpre_ref[...], sh_ref, sc_ref, g), w_ref[...])


def _s5in(x, mods, l, gpre, w_in, groups, tm):
    n, d = x.shape
    row = lambda i: (i, 0)
    const = lambda i: (0, 0)
    return pl.pallas_call(
        functools.partial(_s5in_kernel, groups=groups, tm=tm),
        grid=(n // tm,),
        in_specs=[pl.BlockSpec((tm, d), row)] + _mod_specs(l, 1, (0, 1)) + [
            pl.BlockSpec((1, d), const), pl.BlockSpec((d, d), const)],
        out_specs=pl.BlockSpec((tm, d), row),
        out_shape=jax.ShapeDtypeStruct((n, d), F32),
        compiler_params=_cparams(("parallel",)),
        name="s5_in",
    )(x, mods, mods, gpre.reshape(1, d), w_in)


def _s5_param_kernel(lr_ref, li_ref, dt_ref, br_ref, bi_ref, bbr_ref, bbi_ref, apr_ref, api_ref, *, npow):
    for d in range(2):
        lr, li, dt = lr_ref[d:d + 1, :], li_ref[d:d + 1, :], jnp.exp(dt_ref[d:d + 1, :])
        mag = jnp.exp(lr * dt)
        abr, abi = mag * jnp.cos(li * dt), mag * jnp.sin(li * dt)
        den = lr * lr + li * li
        fr = ((abr - 1) * lr + abi * li) / den
        fi = (abi * lr - (abr - 1) * li) / den
        br, bi = br_ref[d], bi_ref[d]
        bbr_ref[d] = fr * br - fi * bi
        bbi_ref[d] = fr * bi + fi * br
        pr, pi = abr, abi
        for k in range(npow):
            apr_ref[d, k:k + 1, :] = pr
            api_ref[d, k:k + 1, :] = pi
            pr, pi = pr * abr - pi * abi, pr * abi + pi * abr


def _s5_params(lam_re, lam_im, log_dt, b_re, b_im, npow):
    lanes = S5_LANES
    flat = lambda t: t.reshape(2, lanes)
    dt = jnp.repeat(log_dt, S5_STATE, axis=-1)
    to_cl = lambda t: jnp.transpose(t, (0, 3, 1, 2)).reshape(2, S5_GROUP_CH, lanes)
    return pl.pallas_call(
        functools.partial(_s5_param_kernel, npow=npow),
        out_shape=[jax.ShapeDtypeStruct((2, S5_GROUP_CH, lanes), F32)] * 2
        + [jax.ShapeDtypeStruct((2, npow, lanes), F32)] * 2,
        name="s5_params",
    )(flat(lam_re), flat(lam_im), dt, to_cl(b_re), to_cl(b_im))


def _s5_scan_kernel(u_ref, perm_ref, permt_ref, wbr_ref, wbi_ref, wcr_ref, wci_ref, apr_ref, api_ref, s0_ref,
                    y_ref, sfin_ref, up_scr, br_scr, bi_scr, hin_r, hin_i, car_r, car_i,
                    *, reverse, nchunk):
    c = pl.program_id(1)
    tc = u_ref.shape[0]
    nseg = S5_SEGS
    slen = tc // nseg
    lanes = S5_LANES
    piece = 1024

    @pl.when(c == 0)
    def _():
        car_r[...] = s0_ref[0:1, :]
        car_i[...] = s0_ref[1:2, :]

    up_scr[...] = jnp.dot(perm_ref[...], u_ref[...].astype(BF16), preferred_element_type=F32).astype(BF16)
    kw = wbr_ref.shape[1]
    for ks in range(wbr_ref.shape[0]):
        lhs = up_scr[:, ks * kw:(ks + 1) * kw]
        ow = wbr_ref.shape[2]
        br_scr[:, ks * ow:(ks + 1) * ow] = jnp.dot(lhs, wbr_ref[ks], preferred_element_type=F32)
        bi_scr[:, ks * ow:(ks + 1) * ow] = jnp.dot(lhs, wbi_ref[ks], preferred_element_type=F32)

    steps = list(range(slen))
    order = steps[::-1] if reverse else steps
    for p in range(lanes // piece):
        ls = slice(p * piece, (p + 1) * piece)
        ar = jnp.broadcast_to(apr_ref[0:1, ls], (nseg, piece))
        ai = jnp.broadcast_to(api_ref[0:1, ls], (nseg, piece))
        hr = jnp.zeros((nseg, piece), F32)
        hi = jnp.zeros((nseg, piece), F32)
        for step in order:
            rs = slice(step * nseg, (step + 1) * nseg)
            hr, hi = (ar * hr - ai * hi + br_scr[rs, ls], ar * hi + ai * hr + bi_scr[rs, ls])
            br_scr[rs, ls] = hr
            bi_scr[rs, ls] = hi
        a_r, a_i = apr_ref[slen - 1:slen, ls], api_ref[slen - 1:slen, ls]
        cr, ci = car_r[:, ls], car_i[:, ls]
        seg_order = list(range(nseg))[::-1] if reverse else list(range(nseg))
        for seg in seg_order:
            hin_r[seg:seg + 1, ls] = cr
            hin_i[seg:seg + 1, ls] = ci
            er, ei = hr[seg:seg + 1, :], hi[seg:seg + 1, :]
            cr, ci = er + (a_r * cr - a_i * ci), ei + (a_r * ci + a_i * cr)
        car_r[:, ls] = cr
        car_i[:, ls] = ci
        e_r, e_i = hin_r[:, ls], hin_i[:, ls]
        for k, step in enumerate(order):
            rs = slice(step * nseg, (step + 1) * nseg)
            pr, pi = apr_ref[k:k + 1, ls], api_ref[k:k + 1, ls]
            br_scr[rs, ls] += pr * e_r - pi * e_i
            bi_scr[rs, ls] += pr * e_i + pi * e_r

    cw = wcr_ref.shape[1]
    pieces = []
    for j in range(wcr_ref.shape[0]):
        pieces.append(
            jnp.dot(br_scr[:, j * cw:(j + 1) * cw].astype(BF16), wcr_ref[j], preferred_element_type=F32)
            + jnp.dot(bi_scr[:, j * cw:(j + 1) * cw].astype(BF16), wci_ref[j], preferred_element_type=F32))
    y_perm = jnp.concatenate(pieces, axis=1)
    y_ref[...] = functools.reduce(
        lambda a, b: a + b,
        [jnp.dot(permt_ref[...], part, preferred_element_type=F32) for part in _split3(y_perm)])

    @pl.when(c == nchunk - 1)
    def _():
        sfin_ref[0:1, :] = car_r[...]
        sfin_ref[1:2, :] = car_i[...]


def _s5_scan(u, wbr, wbi, wcr, wci, apr, api, s0, row0, nseq, seqlen, reverse):
    tc = min(S5_CHUNK, seqlen)
    nchunk = seqlen // tc
    cb0 = row0 // tc
    slen = tc // S5_SEGS
    perm = np.zeros((tc, tc), np.float32)
    for t in range(tc):
        perm[(t % slen) * S5_SEGS + t // slen, t] = 1.0
    d = u.shape[1]
    lanes = S5_LANES
    if reverse:
        tok = lambda s, c: (cb0 + s * nchunk + nchunk - 1 - c, 0)
        out = lambda s, c: (s * nchunk + nchunk - 1 - c, 0)
    else:
        tok = lambda s, c: (cb0 + s * nchunk + c, 0)
        out = lambda s, c: (s * nchunk + c, 0)
    c3 = lambda s, c: (0, 0, 0)
    c2 = lambda s, c: (0, 0)
    st_spec = pl.BlockSpec((None, 2, lanes), lambda s, c: (s, 0, 0))
    return pl.pallas_call(
        functools.partial(_s5_scan_kernel, reverse=reverse, nchunk=nchunk),
        grid=(nseq, nchunk),
        in_specs=[pl.BlockSpec((tc, d), tok), pl.BlockSpec((tc, tc), c2), pl.BlockSpec((tc, tc), c2),
                  pl.BlockSpec(wbr.shape, c3), pl.BlockSpec(wbi.shape, c3),
                  pl.BlockSpec(wcr.shape, c3), pl.BlockSpec(wci.shape, c3),
                  pl.BlockSpec(apr.shape, c2), pl.BlockSpec(api.shape, c2), st_spec],
        out_specs=[pl.BlockSpec((tc, d), out), st_spec],
        out_shape=[jax.ShapeDtypeStruct((nseq * seqlen, d), F32),
                   jax.ShapeDtypeStruct((nseq, 2, lanes), F32)],
        scratch_shapes=[pltpu.VMEM((tc, d), BF16), pltpu.VMEM((tc, lanes), F32), pltpu.VMEM((tc, lanes), F32),
                        pltpu.VMEM((S5_SEGS, lanes), F32), pltpu.VMEM((S5_SEGS, lanes), F32),
                        pltpu.VMEM((1, lanes), F32), pltpu.VMEM((1, lanes), F32)],
        compiler_params=_cparams(("parallel", "arbitrary")),
        name="s5_bwd" if reverse else "s5_fwd",
    )(u, jnp.asarray(perm, BF16), jnp.asarray(perm.T, BF16), wbr, wbi, wcr, wci, apr, api, s0)


def _s5post_kernel(x_ref, gt_ref, gpost_ref, u_ref, yf_ref, yb_ref, dsk_ref, wg_ref, wo_ref, o_ref,
                   *, groups, tm):
    g = groups.mod_row(pl.program_id(0) * tm)
    ys = (yf_ref[...] + yb_ref[...]) + dsk_ref[...] * u_ref[...]
    z = jax.nn.gelu(ys)
    z = z * jax.nn.sigmoid(_bdot(z, wg_ref[...]))
    o_ref[...] = _residual(x_ref[...], _bdot(z, wo_ref[...]), gpost_ref[...], gt_ref, g, 1.0)


def _s5post(x, mods, l, gpost, u, yf, yb, d_skip, w_glu, w_out, groups, tm):
    n, d = x.shape
    row = lambda i: (i, 0)
    const = lambda i: (0, 0)
    return pl.pallas_call(
        functools.partial(_s5post_kernel, groups=groups, tm=tm),
        grid=(n // tm,),
        in_specs=[pl.BlockSpec((tm, d), row)] + _mod_specs(l, 1, (2,)) + [
            pl.BlockSpec((1, d), const)] + [pl.BlockSpec((tm, d), row)] * 3 + [
            pl.BlockSpec((1, d), const), pl.BlockSpec((d, d), const), pl.BlockSpec((d, d), const)],
        out_specs=pl.BlockSpec((tm, d), row),
        out_shape=jax.ShapeDtypeStruct((n, d), F32),
        compiler_params=_cparams(("parallel",)),
        name="s5_post",
    )(x, mods, gpost.reshape(1, d), u, yf, yb, d_skip.reshape(1, d), w_glu, w_out)


def _block_diag(blocks):
    n, r, c = blocks.shape
    eye = jnp.eye(n, dtype=blocks.dtype)
    return (eye[:, None, :, None] * blocks[:, :, None, :]).reshape(n * r, n * c)


def _s5_weights(bbr, bbi, c_re, c_im):
    gk = 256 // S5_GROUP_CH
    gc = 512 // S5_STATE

    def b_weight(bb):
        t = jnp.transpose(bb.reshape(S5_GROUP_CH, S5_GROUPS, S5_STATE), (1, 0, 2))
        t = t.reshape(S5_GROUPS // gk, gk, S5_GROUP_CH, S5_STATE)
        return jax.vmap(_block_diag)(t).astype(BF16)

    def c_weight(cc):
        t = jnp.transpose(cc, (0, 2, 1)).reshape(S5_GROUPS // gc, gc, S5_STATE, S5_GROUP_CH)
        return jax.vmap(_block_diag)(t).astype(BF16)

    return b_weight(bbr), b_weight(bbi), c_weight(c_re), c_weight(-c_im)


def _rope_tables(groups):
    quarter = HEAD_DIM // 4
    t = jnp.arange(groups.ls)
    rowp = (t // GRID_W).astype(F32)
    colp = (t % GRID_W).astype(F32)
    inv = ROPE_THETA ** (-jnp.arange(quarter, dtype=F32) * 2.0 / (HEAD_DIM // 2))
    ar, ac = rowp[:, None] * inv, colp[:, None] * inv
    ang = jnp.concatenate([ar, ar, ac, ac], axis=-1)
    cos, sin = jnp.cos(ang), jnp.sin(ang)
    sign = jnp.tile(jnp.concatenate([-jnp.ones(quarter, F32), jnp.ones(quarter, F32)]), 2)
    sin = sin * sign
    cos = jnp.concatenate([jnp.ones((groups.np_rows, HEAD_DIM), F32), jnp.tile(cos, (groups.bs, 1))], 0)
    sin = jnp.concatenate([jnp.zeros((groups.np_rows, HEAD_DIM), F32), jnp.tile(sin, (groups.bs, 1))], 0)
    return jnp.tile(cos, (1, 2)), jnp.tile(sin, (1, 2))


def kernel(x_prompt, x_sample, cache_k, cache_v, state_rwkv, state_s5, c, c_ctx, ada_w, ada_b, norm_pre, norm_post, ffn_w1, ffn_w3, ffn_w2, ab_w_in, ab_w_out, attn_q_gain, attn_k_gain, rwkv_mu, rwkv_w0, rwkv_w_up, rwkv_a0, rwkv_a_up, rwkv_g_up, rwkv_k_k, rwkv_k_a, rwkv_r_k, rwkv_ln_w, rwkv_ln_b, s5_w_in, s5_lambda_re, s5_lambda_im, s5_log_dt, s5_b_re, s5_b_im, s5_c_re, s5_c_im, s5_d, s5_w_glu, s5_w_out):
    bp, lp, d = x_prompt.shape
    bs, ls, _ = x_sample.shape
    groups = _Groups(bp, lp, bs, ls)
    n, np_rows = groups.n_rows, groups.np_rows
    depth = ada_w.shape[0]
    tm_ffn = math.gcd(1024, math.gcd(np_rows, ls))
    tm_tok = math.gcd(512, math.gcd(lp, ls))
    tm_shift = math.gcd(256, math.gcd(lp, ls))

    x = jnp.concatenate([x_prompt.reshape(np_rows, d), x_sample.reshape(bs * ls, d)], axis=0)
    cvec = jnp.concatenate([c_ctx[None], c, jnp.zeros((8 - 1 - bs, d), F32)], axis=0)
    mods = _adaln(cvec, ada_w, ada_b)

    cos_t, sin_t = _rope_tables(groups)
    ones_bd = jnp.asarray(np.kron(np.eye(RWKV_HEADS), np.ones((RWKV_HEAD, RWKV_HEAD))), BF16)
    heads_shape = (RWKV_HEADS, RWKV_HEAD, RWKV_HEAD)
    new_k, new_v, new_rwkv, new_s5 = [], [], [], []

    for l in range(depth):
        i = l // 2
        x = _ffn(x, mods, l, 0, norm_pre[l, 0], norm_post[l, 0], ffn_w1[l, 0], ffn_w3[l, 0], ffn_w2[l, 0],
                 groups, tm_ffn)
        if l % 2 == 0:
            q, k, v, pb = _abin(x, mods, l, norm_pre[l, 1], ab_w_in[i], attn_q_gain[i], attn_k_gain[i],
                                cos_t, sin_t, ones_bd, groups, tm_tok)
            new_k.append(k[:np_rows].reshape(bp, lp, ATTN_KV_HEADS, HEAD_DIM))
            new_v.append(v[:np_rows].reshape(bp, lp, ATTN_KV_HEADS, HEAD_DIM))
            past = cache_k.shape[2]
            oa_p = _attention(q, k, v, None, None, 0, bp, lp)
            oa_s = _attention(q, k, v, cache_k[:, i].reshape(bs, past, ATTN_KV_WIDTH),
                              cache_v[:, i].reshape(bs, past, ATTN_KV_WIDTH), np_rows, bs, ls)
            oa = jnp.concatenate([oa_p, oa_s], axis=0)

            (r, vv, kk, gate, bonus, lwf, lwb, kdf, kdb, bf, bb) = _rwkv_prep(
                pb, rwkv_mu[i], rwkv_w0[i], rwkv_w_up[i], rwkv_a0[i], rwkv_a_up[i], rwkv_g_up[i],
                rwkv_k_k[i], rwkv_k_a[i], rwkv_r_k[i].reshape(-1), ones_bd, groups, tm_shift)
            zero_state = jnp.zeros((bp,) + heads_shape, F32)
            ys, fins = [], []
            for dr, (lw, kd, b) in enumerate(((lwf, kdf, bf), (lwb, kdb, bb))):
                y_p, fin = _wkv(r, vv, kk, lw, kd, b, zero_state, 0, bp, lp, dr == 1)
                y_s, _ = _wkv(r, vv, kk, lw, kd, b, state_rwkv[:, i, dr], np_rows, bs, ls, dr == 1)
                ys.append(jnp.concatenate([y_p, y_s], axis=0))
                fins.append(fin)
            new_rwkv.append(jnp.stack(fins, axis=1))
            x = _about(x, mods, l, norm_post[l, 1], oa, ys[0], ys[1], bonus, gate, rwkv_ln_w[i],
                       rwkv_ln_b[i], ones_bd, ab_w_out[i], groups, tm_tok)
        else:
            u = _s5in(x, mods, l, norm_pre[l, 1], s5_w_in[i], groups, tm_tok)
            tc = min(S5_CHUNK, lp, ls)
            bbr, bbi, apr, api = _s5_params(s5_lambda_re[i], s5_lambda_im[i], s5_log_dt[i],
                                            s5_b_re[i], s5_b_im[i], tc // S5_SEGS)
            ys, fins = [], []
            for dr in range(2):
                wts = _s5_weights(bbr[dr], bbi[dr], s5_c_re[i, dr], s5_c_im[i, dr])
                zero_state = jnp.zeros((bp, 2, S5_LANES), F32)
                y_p, fin = _s5_scan(u, *wts, apr[dr], api[dr], zero_state, 0, bp, lp, dr == 1)
                y_s, _ = _s5_scan(u, *wts, apr[dr], api[dr], state_s5[:, i, dr].reshape(bs, 2, S5_LANES),
                                  np_rows, bs, ls, dr == 1)
                ys.append(jnp.concatenate([y_p, y_s], axis=0))
                fins.append(fin.reshape(bp, 2, S5_GROUPS, S5_STATE))
            new_s5.append(jnp.stack(fins, axis=1))
            x = _s5post(x, mods, l, norm_post[l, 1], u, ys[0], ys[1], s5_d[i], s5_w_glu[i], s5_w_out[i],
                        groups, tm_tok)
        x = _ffn(x, mods, l, 2, norm_pre[l, 2], norm_post[l, 2], ffn_w1[l, 1], ffn_w3[l, 1], ffn_w2[l, 1],
                 groups, tm_ffn)

    y_prompt = x[:np_rows].reshape(bp, lp, d)
    y_sample = x[np_rows:].reshape(bs, ls, d)
    return (y_prompt, y_sample, jnp.stack(new_k, 1), jnp.stack(new_v, 1), jnp.stack(new_rwkv, 1),
            jnp.stack(new_s5, 1))
```
